```python
import math
import jax, jax.numpy as jnp
from jax import lax
import numpy as np

D_MODEL = 1024
BATCH = 4
SEQ = 8192
DEPTH = 1
DEC_BATCH = 128
DEC_SEQ = 1
PAST_LEN = 16384
PAGE_SIZE = 128

HEAD_DIM = 64
N_HEADS = D_MODEL // HEAD_DIM
SWA_Q_HEADS = N_HEADS // 4
SWA_KV_HEADS = SWA_Q_HEADS // 2
SWA_GROUP = SWA_Q_HEADS // SWA_KV_HEADS
SWA_WINDOW = 128
DIL_CONFIGS = ((128, 1), (512, 4), (2048, 16))
N_DIL = len(DIL_CONFIGS)
DIL_HEADS = (N_HEADS - SWA_Q_HEADS) // N_DIL
SWA_Q_WIDTH = SWA_Q_HEADS * HEAD_DIM
SWA_KV_WIDTH = SWA_KV_HEADS * HEAD_DIM
DIL_WIDTH = N_DIL * DIL_HEADS * HEAD_DIM
QKV_SPLITS = (SWA_Q_WIDTH, SWA_Q_WIDTH + SWA_KV_WIDTH, SWA_Q_WIDTH + 2 * SWA_KV_WIDTH,
              SWA_Q_WIDTH + 2 * SWA_KV_WIDTH + DIL_WIDTH,
              SWA_Q_WIDTH + 2 * SWA_KV_WIDTH + 2 * DIL_WIDTH)
QKV_WIDTH = SWA_Q_WIDTH + 2 * SWA_KV_WIDTH + 3 * DIL_WIDTH
MIX_OUT = (SWA_Q_HEADS + DIL_HEADS) * HEAD_DIM
NUM_BUCKETS = 32
MAX_EXACT = NUM_BUCKETS // 2
MAX_DISTANCE = 2048
N_EXPERTS = 32
TOP_K = 4
D_FF = D_MODEL
SWIGLU_ALPHA = 1.702
SWIGLU_LIMIT = 7.0
MOE_BLOCK = 128
BAND_BLOCK = 128
LN_EPS = 1e-5
DEEPNORM_ALPHA = (2 * DEPTH) ** 0.25
DEEPNORM_BETA = (8 * DEPTH) ** -0.25

kernel_name = 'hybrid_swa_dilated_moe_step'


def layer_norm(x, g, b):
    xf = x.astype(jnp.float32)
    mu = xf.mean(-1, keepdims=True)
    var = jnp.square(xf - mu).mean(-1, keepdims=True)
    return ((xf - mu) * lax.rsqrt(var + LN_EPS) * g + b).astype(x.dtype)


def t5_bucket(dist):
    n = jnp.maximum(dist, 0)
    ratio = jnp.log(jnp.maximum(n, MAX_EXACT).astype(jnp.float32) / MAX_EXACT) / math.log(MAX_DISTANCE / MAX_EXACT)
    large = MAX_EXACT + (ratio * (NUM_BUCKETS - MAX_EXACT)).astype(jnp.int32)
    return jnp.where(n < MAX_EXACT, n, jnp.minimum(large, NUM_BUCKETS - 1))


def attend(q, k, v, dist, valid, bias_table, sink=None):
    kvh, grp, hd = q.shape[-3:]
    s = jnp.einsum('...qhgd,...shd->...hgqs', q, k).astype(jnp.float32) * (hd ** -0.5)
    bias = bias_table[t5_bucket(dist)].astype(jnp.float32)
    bias = bias.reshape(bias.shape[:-1] + (kvh, grp))
    bias = jnp.moveaxis(bias, (-2, -1), (-4, -3))
    s = jnp.where(valid[..., None, None, :, :], s + bias, -jnp.inf)
    m = s.max(axis=-1, keepdims=True)
    if sink is not None:
        sk = sink.astype(jnp.float32)[:, :, None, None]
        m = jnp.maximum(m, sk)
    e = jnp.exp(s - m)
    denom = e.sum(axis=-1, keepdims=True)
    if sink is not None:
        denom = denom + jnp.exp(sk - m)
    p = (e / denom).astype(v.dtype)
    out = jnp.einsum('...hgqs,...shd->...qhgd', p, v)
    lse = jnp.moveaxis((m + jnp.log(denom))[..., 0], -1, -3)
    return out, lse


def banded_attention(q, k, v, max_dist, dilation, bias_table, sink=None):
    n, L = q.shape[:2]
    nb = -(-L // BAND_BLOCK)
    lp = nb * BAND_BLOCK
    def pad_seq(x):
        return jnp.pad(x, [(0, 0), (0, lp - L)] + [(0, 0)] * (x.ndim - 2))
    def with_prev(x):
        xb = pad_seq(x).reshape((n, nb, BAND_BLOCK) + x.shape[2:])
        prev = jnp.pad(xb[:, :-1], [(0, 0), (1, 0)] + [(0, 0)] * (xb.ndim - 2))
        return jnp.concatenate([prev, xb], axis=2)
    qb = pad_seq(q).reshape((n, nb, BAND_BLOCK) + q.shape[2:])
    kk, vv = with_prev(k), with_prev(v)
    i = jnp.arange(BAND_BLOCK)[:, None]
    j = jnp.arange(2 * BAND_BLOCK)[None, :]
    dist = BAND_BLOCK + i - j
    kpos = (jnp.arange(nb)[:, None, None] - 1) * BAND_BLOCK + j
    valid = (dist >= 0) & (dist <= max_dist) & (kpos >= 0)
    out, lse = attend(qb, kk, vv, dist * dilation, valid, bias_table, sink)
    out = out.reshape((n, lp) + out.shape[3:])[:, :L]
    lse = lse.reshape((n, lp) + lse.shape[3:])[:, :L]
    return out, lse


def to_strided(x, d):
    n, s = x.shape[:2]
    x = jnp.swapaxes(x.reshape((n, s // d, d) + x.shape[2:]), 1, 2)
    return x.reshape((n * d, s // d) + x.shape[3:])


def from_strided(x, d, n):
    nd, l = x.shape[:2]
    x = jnp.swapaxes(x.reshape((n, d, l) + x.shape[2:]), 1, 2)
    return x.reshape((n, l * d) + x.shape[3:])


def gathered_attention(q, kv_cat, n_off, dilation, bias_table, sink=None):
    t = q.shape[1]
    l_past = kv_cat.shape[1] - t
    off = jnp.arange(n_off)
    idx = l_past + jnp.arange(t)[:, None] - off[None, :] * dilation
    valid = idx >= 0
    kvg = kv_cat[:, jnp.maximum(idx, 0)]
    out, lse = attend(q[:, :, None], kvg[..., 0, :, :], kvg[..., 1, :, :],
                      (off * dilation)[None, :], valid[:, None, :], bias_table, sink)
    return out[:, :, 0], lse[:, :, 0]


def project_qkv(h, w_in, b_in):
    n, t = h.shape[:2]
    p = h @ w_in + b_in
    qa, ka, va, qd, kd, vd = jnp.split(p, list(QKV_SPLITS), axis=-1)
    qa = qa.reshape(n, t, SWA_KV_HEADS, SWA_GROUP, HEAD_DIM)
    kva = jnp.stack([ka.reshape(n, t, SWA_KV_HEADS, HEAD_DIM),
                     va.reshape(n, t, SWA_KV_HEADS, HEAD_DIM)], axis=2)
    qd = qd.reshape(n, t, N_DIL, DIL_HEADS, 1, HEAD_DIM)
    kvd = jnp.stack([kd.reshape(n, t, N_DIL, DIL_HEADS, HEAD_DIM),
                     vd.reshape(n, t, N_DIL, DIL_HEADS, HEAD_DIM)], axis=3)
    return qa, kva, qd, kvd


def dil_table(table, g):
    return table[:, SWA_Q_HEADS + g * DIL_HEADS: SWA_Q_HEADS + (g + 1) * DIL_HEADS]


def merge_project(oa, outs, lses, w_o, b_o):
    n, t = oa.shape[:2]
    o = jnp.stack(outs)[..., 0, :]
    lse = jnp.stack(lses)[..., 0]
    wts = jax.nn.softmax(lse, axis=0).astype(o.dtype)
    od = jnp.einsum('gnth,gnthd->nthd', wts, o)
    cat = jnp.concatenate([oa.reshape(n, t, -1), od.reshape(n, t, -1)], axis=-1)
    return cat @ w_o + b_o


def mix_prompt(h, w_in, b_in, sinks, table, w_o, b_o):
    n, s = h.shape[:2]
    qa, kva, qd, kvd = project_qkv(h, w_in, b_in)
    oa, _ = banded_attention(qa, kva[:, :, 0], kva[:, :, 1], SWA_WINDOW - 1, 1,
                             table[:, :SWA_Q_HEADS], sinks)
    states = [kva[:, -min(SWA_WINDOW, s):]]
    outs, lses = [], []
    for g, (win, dil) in enumerate(DIL_CONFIGS):
        o, l = banded_attention(to_strided(qd[:, :, g], dil), to_strided(kvd[:, :, g, 0], dil),
                                to_strided(kvd[:, :, g, 1], dil), win // dil, dil, dil_table(table, g))
        outs.append(from_strided(o, dil, n))
        lses.append(from_strided(l, dil, n))
        states.append(kvd[:, -min(win, s):, g])
    return merge_project(oa, outs, lses, w_o, b_o), states


def mix_sample(h, caches, w_in, b_in, sinks, table, w_o, b_o):
    qa, kva, qd, kvd = project_qkv(h, w_in, b_in)
    kv_cat = jnp.concatenate([caches[0], kva], axis=1)
    oa, _ = gathered_attention(qa, kv_cat, SWA_WINDOW, 1, table[:, :SWA_Q_HEADS], sinks)
    states = [kv_cat[:, -min(SWA_WINDOW, kv_cat.shape[1]):]]
    outs, lses = [], []
    for g, (win, dil) in enumerate(DIL_CONFIGS):
        kv_cat = jnp.concatenate([caches[g + 1], kvd[:, :, g]], axis=1)
        o, l = gathered_attention(qd[:, :, g], kv_cat, win // dil + 1, dil, dil_table(table, g))
        outs.append(o)
        lses.append(l)
        states.append(kv_cat[:, -min(win, kv_cat.shape[1]):])
    return merge_project(oa, outs, lses, w_o, b_o), states


def moe_tokens(x2d, w_router, b_router, w_up, b_up, w_down, b_down):
    t = x2d.shape[0]
    logits = (x2d @ w_router + b_router).astype(jnp.float32)
    top_val, top_idx = lax.top_k(logits, TOP_K)
    gates = jax.nn.softmax(top_val, axis=-1).astype(x2d.dtype)
    flat_e = top_idx.reshape(-1)
    flat_tok = jnp.repeat(jnp.arange(t, dtype=jnp.int32), TOP_K)
    flat_g = gates.reshape(-1)
    order = jnp.argsort(flat_e)
    se = flat_e[order]
    counts = jnp.bincount(flat_e, length=N_EXPERTS)
    padded = (counts + MOE_BLOCK - 1) // MOE_BLOCK * MOE_BLOCK
    start = jnp.cumsum(counts) - counts
    pend = jnp.cumsum(padded)
    pstart = pend - padded
    dest = pstart[se] + jnp.arange(t * TOP_K) - start[se]
    n_blocks = -(-(t * TOP_K) // MOE_BLOCK) + N_EXPERTS
    rows = n_blocks * MOE_BLOCK
    row_tok = jnp.full((rows,), t, jnp.int32).at[dest].set(flat_tok[order])
    row_gate = jnp.zeros((rows,), x2d.dtype).at[dest].set(flat_g[order])
    block_e = jnp.minimum(jnp.searchsorted(pend, jnp.arange(n_blocks) * MOE_BLOCK, side='right'),
                          N_EXPERTS - 1)
    def expert_block(args):
        e, toks, g = args
        xb = x2d[jnp.minimum(toks, t - 1)]
        hb = xb @ w_up[e] + b_up[e]
        glu = jnp.minimum(hb[:, :D_FF], SWIGLU_LIMIT)
        lin = jnp.clip(hb[:, D_FF:], -SWIGLU_LIMIT, SWIGLU_LIMIT)
        a = glu * jax.nn.sigmoid(SWIGLU_ALPHA * glu) * (lin + 1)
        return (a @ w_down[e] + b_down[e]) * g[:, None]
    y = lax.map(expert_block, (block_e, row_tok.reshape(n_blocks, MOE_BLOCK),
                               row_gate.reshape(n_blocks, MOE_BLOCK)))
    return jax.ops.segment_sum(y.reshape(rows, -1), row_tok, num_segments=t + 1)[:t]


def moe_ffn(x, w_router, b_router, w_up, b_up, w_down, b_down):
    n, t, d = x.shape
    return moe_tokens(x.reshape(n * t, d), w_router, b_router, w_up, b_up, w_down, b_down).reshape(n, t, d)


def setup_inputs(seed: int = 0) -> dict:
    key = jax.random.key(seed)
    ks = jax.random.split(key, 24)
    f32 = jnp.float32
    def nrm(k, shape, scale):
        return jax.random.normal(k, shape, f32) * scale
    def kv_shape(win, heads):
        return (DEPTH, DEC_BATCH, min(win, PAST_LEN), 2, heads, HEAD_DIM)
    return {
        'x_prompt': nrm(ks[0], (BATCH, SEQ, D_MODEL), 1.0),
        'x_sample': nrm(ks[1], (DEC_BATCH, DEC_SEQ, D_MODEL), 1.0),
        'cache_swa_kv': nrm(ks[2], kv_shape(SWA_WINDOW, SWA_KV_HEADS), 1.0),
        'cache_dil1_kv': nrm(ks[3], kv_shape(DIL_CONFIGS[0][0], DIL_HEADS), 1.0),
        'cache_dil2_kv': nrm(ks[4], kv_shape(DIL_CONFIGS[1][0], DIL_HEADS), 1.0),
        'cache_dil3_kv': nrm(ks[5], kv_shape(DIL_CONFIGS[2][0], DIL_HEADS), 1.0),
        'rel_bias_table': nrm(ks[6], (NUM_BUCKETS, N_HEADS), 0.5),
        'w_in': nrm(ks[7], (DEPTH, D_MODEL, QKV_WIDTH), D_MODEL ** -0.5),
        'b_in': nrm(ks[8], (DEPTH, QKV_WIDTH), 0.02),
        'attn_sinks': nrm(ks[9], (DEPTH, SWA_Q_HEADS), 0.5),
        'w_o': nrm(ks[10], (DEPTH, MIX_OUT, D_MODEL), MIX_OUT ** -0.5 * DEEPNORM_BETA),
        'b_o': nrm(ks[11], (DEPTH, D_MODEL), 0.02),
        'ln1_g': 1.0 + nrm(ks[12], (DEPTH, D_MODEL), 0.02),
        'ln1_b': nrm(ks[13], (DEPTH, D_MODEL), 0.02),
        'w_router': nrm(ks[14], (DEPTH, D_MODEL, N_EXPERTS), D_MODEL ** -0.5),
        'b_router': nrm(ks[15], (DEPTH, N_EXPERTS), 0.01),
        'w_up': nrm(ks[16], (DEPTH, N_EXPERTS, D_MODEL, 2 * D_FF), D_MODEL ** -0.5),
        'b_up': nrm(ks[17], (DEPTH, N_EXPERTS, 2 * D_FF), 0.02),
        'w_down': nrm(ks[18], (DEPTH, N_EXPERTS, D_FF, D_MODEL), D_FF ** -0.5 * DEEPNORM_BETA),
        'b_down': nrm(ks[19], (DEPTH, N_EXPERTS, D_MODEL), 0.02),
        'ln2_g': 1.0 + nrm(ks[20], (DEPTH, D_MODEL), 0.02),
        'ln2_b': nrm(ks[21], (DEPTH, D_MODEL), 0.02),
    }


def reference(x_prompt, x_sample, cache_swa_kv, cache_dil1_kv, cache_dil2_kv, cache_dil3_kv,
              rel_bias_table, w_in, b_in, attn_sinks, w_o, b_o, ln1_g, ln1_b,
              w_router, b_router, w_up, b_up, w_down, b_down, ln2_g, ln2_b):
    xp, xs = x_prompt, x_sample
    states_p = [[] for _ in range(1 + N_DIL)]
    states_s = [[] for _ in range(1 + N_DIL)]
    for l in range(DEPTH):
        sinks = attn_sinks[l].reshape(SWA_KV_HEADS, SWA_GROUP)
        mp, sp = mix_prompt(xp, w_in[l], b_in[l], sinks, rel_bias_table, w_o[l], b_o[l])
        caches = [cache_swa_kv[l], cache_dil1_kv[l], cache_dil2_kv[l], cache_dil3_kv[l]]
        ms, ss = mix_sample(xs, caches, w_in[l], b_in[l], sinks, rel_bias_table, w_o[l], b_o[l])
        xp = layer_norm(DEEPNORM_ALPHA * xp + mp, ln1_g[l], ln1_b[l])
        xs = layer_norm(DEEPNORM_ALPHA * xs + ms, ln1_g[l], ln1_b[l])
        xp = layer_norm(DEEPNORM_ALPHA * xp + moe_ffn(xp, w_router[l], b_router[l], w_up[l], b_up[l],
                                                       w_down[l], b_down[l]), ln2_g[l], ln2_b[l])
        xs = layer_norm(DEEPNORM_ALPHA * xs + moe_ffn(xs, w_router[l], b_router[l], w_up[l], b_up[l],
                                                       w_down[l], b_down[l]), ln2_g[l], ln2_b[l])
        for i in range(1 + N_DIL):
            states_p[i].append(sp[i])
            states_s[i].append(ss[i])
    swa_kv_prompt = jnp.stack(states_p[0])
    dil1_kv_prompt = jnp.stack(states_p[1])
    dil2_kv_prompt = jnp.stack(states_p[2])
    dil3_kv_prompt = jnp.stack(states_p[3])
    swa_kv_sample = jnp.stack(states_s[0])
    dil1_kv_sample = jnp.stack(states_s[1])
    dil2_kv_sample = jnp.stack(states_s[2])
    dil3_kv_sample = jnp.stack(states_s[3])
    return (xp, xs, swa_kv_prompt, dil1_kv_prompt, dil2_kv_prompt, dil3_kv_prompt,
            swa_kv_sample, dil1_kv_sample, dil2_kv_sample, dil3_kv_sample)
```

```python
import functools
import math

import jax
import jax.numpy as jnp
import numpy as np
from jax import lax
from jax.experimental import pallas as pl
from jax.experimental.pallas import tpu as pltpu

F32 = jnp.float32
BF16 = jnp.bfloat16
I32 = jnp.int32

HEAD_DIM = 64
GROUP_HEADS = 4
GROUP_WIDTH = GROUP_HEADS * HEAD_DIM
SLAB = 3 * GROUP_WIDTH
N_GROUPS = 4
P_WIDTH = N_GROUPS * SLAB
BAND = 128
SWA_WINDOW = 128
DIL_CONFIGS = ((128, 1), (512, 4), (2048, 16))
GROUPS = ((SWA_WINDOW, 1, SWA_WINDOW - 1, True),) + tuple(
    (w, d, w // d, False) for (w, d) in DIL_CONFIGS)
SWA_HEAD_ORDER = (0, 2, 1, 3)
NUM_BUCKETS = 32
MAX_EXACT = NUM_BUCKETS // 2
MAX_DISTANCE = 2048
N_EXPERTS = 32
TOP_K = 4
SWIGLU_ALPHA = 1.702
SWIGLU_LIMIT = 7.0
LN_EPS = 1e-5
NEG_INF = float("-inf")

VMEM_LIMIT = 56 * 1024 * 1024
QKV_TILE = 512
ATTN_TILE = 512
MERGE_TILE = 512
MOE_TILE = 512
MOE_FF_CHUNK = 512
DISPATCH_TILE = 256
COMBINE_TILE = 256
ROW_CHUNKS = 8
META_LANES = 128


def _cparams(n_axes):
    return pltpu.CompilerParams(dimension_semantics=("arbitrary",) * n_axes,
                                vmem_limit_bytes=VMEM_LIMIT)


def _t5_bucket(dist):
    n = jnp.maximum(dist, 0)
    ratio = jnp.log(jnp.maximum(n, MAX_EXACT).astype(F32) / MAX_EXACT) / math.log(MAX_DISTANCE / MAX_EXACT)
    large = MAX_EXACT + (ratio * (NUM_BUCKETS - MAX_EXACT)).astype(I32)
    return jnp.where(n < MAX_EXACT, n, jnp.minimum(large, NUM_BUCKETS - 1))


def _layer_norm(h, g, b):
    mu = jnp.mean(h, axis=-1, keepdims=True)
    c = h - mu
    var = jnp.mean(c * c, axis=-1, keepdims=True)
    return c * lax.rsqrt(var + LN_EPS) * g + b


def _rows_to_lanes(ref):
    return jnp.concatenate([ref[:, c, :] for c in range(ROW_CHUNKS)], axis=1)


def _lanes_to_rows(ref, x):
    for c in range(ROW_CHUNKS):
        ref[:, c, :] = x[:, c * 128:(c + 1) * 128]


def _head_of_lane(shape, axis):
    return lax.broadcasted_iota(I32, shape, axis) // HEAD_DIM


def _qkv_prompt_kernel(x_ref, w_ref, b_ref, wkva_ref, bkva_ref,
                       p_ref, t_swa, t_d1, t_d2, t_d3, *, nt, tm):
    i = pl.program_id(1)
    xb = x_ref[0].astype(BF16)
    tails = (None, t_d1, t_d2, t_d3)
    for g in range(N_GROUPS):
        sl = slice(g * SLAB, (g + 1) * SLAB)
        acc = jnp.dot(xb, w_ref[:, sl], preferred_element_type=F32) + b_ref[:, sl]
        p_ref[0, :, sl] = acc.astype(BF16)
        if tails[g] is not None:
            win = GROUPS[g][0]
            rows = min(win, tm)
            ntail = max(win // tm, 1)
            t_ref = tails[g]

            @pl.when(i >= nt - ntail)
            def _(acc=acc, t_ref=t_ref, rows=rows):
                t_ref[0] = acc[tm - rows:, GROUP_WIDTH:]

    @pl.when(i == nt - 1)
    def _():
        t_swa[0] = (jnp.dot(xb[tm - SWA_WINDOW:], wkva_ref[...], preferred_element_type=F32)
                    + bkva_ref[...])


def _qkv_prompt(x, wp, bp, wkva, bkva):
    B, S, D = x.shape
    tm = QKV_TILE
    assert S % tm == 0 and S >= max(g[0] for g in GROUPS)
    nt = S // tm

    def tail_spec(win, width):
        rows = min(win, tm)
        ntail = max(win // tm, 1)
        return pl.BlockSpec((1, rows, width),
                            lambda b, i: (b, jnp.maximum(i - (nt - ntail), 0), 0))

    out_shape = [jax.ShapeDtypeStruct((B, S, P_WIDTH), BF16),
                 jax.ShapeDtypeStruct((B, SWA_WINDOW, GROUP_WIDTH), F32)]
    out_specs = [pl.BlockSpec((1, tm, P_WIDTH), lambda b, i: (b, i, 0)),
                 tail_spec(SWA_WINDOW, GROUP_WIDTH)]
    for (w, _) in DIL_CONFIGS:
        out_shape.append(jax.ShapeDtypeStruct((B, w, 2 * GROUP_WIDTH), F32))
        out_specs.append(tail_spec(w, 2 * GROUP_WIDTH))
    return pl.pallas_call(
        functools.partial(_qkv_prompt_kernel, nt=nt, tm=tm),
        grid=(B, nt),
        in_specs=[pl.BlockSpec((1, tm, D), lambda b, i: (b, i, 0)),
                  pl.BlockSpec((D, P_WIDTH), lambda b, i: (0, 0)),
                  pl.BlockSpec((1, P_WIDTH), lambda b, i: (0, 0)),
                  pl.BlockSpec((D, GROUP_WIDTH), lambda b, i: (0, 0)),
                  pl.BlockSpec((1, GROUP_WIDTH), lambda b, i: (0, 0))],
        out_specs=out_specs,
        out_shape=out_shape,
        compiler_params=_cparams(2),
        name="qkv_prompt",
    )(x, wp, bp, wkva, bkva)


def _build_bias(bucket, table_ref, n_heads):
    accs = [jnp.full(bucket.shape, NEG_INF, F32) for _ in range(n_heads)]
    for bk in range(NUM_BUCKETS):
        hit = bucket == bk
        for h in range(n_heads):
            accs[h] = jnp.where(hit, table_ref[bk * GROUP_HEADS + h], accs[h])
    return accs


def _band_attn_kernel(table_ref, sink_ref, bucket_ref, q_ref, kc_ref, vc_ref, kp_ref, vp_ref,
                      o_ref, lse_ref, bias_scr, sink_scr, kbuf, vbuf, *, tq, has_sink):
    b, r, j = pl.program_id(0), pl.program_id(1), pl.program_id(2)
    H = GROUP_HEADS

    @pl.when((b == 0) & (r == 0) & (j == 0))
    def _():
        accs = _build_bias(bucket_ref[...], table_ref, H)
        for h in range(H):
            bias_scr[h * BAND:(h + 1) * BAND, :] = accs[h]
        row_head = lax.broadcasted_iota(I32, (H * BAND, 128), 0) // BAND
        sk = jnp.full((H * BAND, 128), NEG_INF, F32)
        if has_sink:
            for h in range(H):
                sk = jnp.where(row_head == h, sink_ref[h], sk)
        sink_scr[...] = sk

    kbuf[0:BAND] = kp_ref[0]
    kbuf[BAND:] = kc_ref[0]
    vbuf[0:BAND] = vp_ref[0]
    vbuf[BAND:] = vc_ref[0]
    first = j == 0
    lane_head = _head_of_lane((BAND, GROUP_WIDTH), 1)
    key_head = _head_of_lane((2 * BAND, GROUP_WIDTH), 1)
    col = lax.broadcasted_iota(I32, (H * BAND, 2 * BAND), 1)
    zero_q = jnp.zeros((BAND, GROUP_WIDTH), BF16)
    zero_v = jnp.zeros((2 * BAND, GROUP_WIDTH), BF16)
    for sb in range(tq // BAND):
        rows = slice(sb * BAND, (sb + 1) * BAND)
        q = q_ref[0, rows, :]
        qs = jnp.concatenate([jnp.where(lane_head == h, q, zero_q) for h in range(H)], axis=0)
        kk = kbuf[sb * BAND: sb * BAND + 2 * BAND]
        s = lax.dot_general(qs, kk, (((1,), (1,)), ((), ())), preferred_element_type=F32)
        s = s + bias_scr[...]
        if sb == 0:
            s = jnp.where(first & (col < BAND), NEG_INF, s)
        m = jnp.max(s, axis=-1, keepdims=True)
        if has_sink:
            sk = sink_scr[:, 0:1]
            m = jnp.maximum(m, sk)
        e = jnp.exp(s - m)
        den = jnp.sum(e, axis=-1, keepdims=True)
        if has_sink:
            den = den + jnp.exp(sk - m)
        p = (e * (1.0 / den)).astype(BF16)
        pc = jnp.concatenate([p[h * BAND:(h + 1) * BAND] for h in range(H)], axis=1)
        vv = vbuf[sb * BAND: sb * BAND + 2 * BAND]
        vbd = jnp.concatenate([jnp.where(key_head == h, vv, zero_v) for h in range(H)], axis=0)
        o = jnp.dot(pc, vbd, preferred_element_type=F32)
        o_ref[0, rows, :] = o.astype(o_ref.dtype)
        lse = m + jnp.log(den)
        lse_x = jnp.zeros((BAND, GROUP_WIDTH), F32)
        for h in range(H):
            lse_x = jnp.where(lane_head == h, lse[h * BAND:(h + 1) * BAND], lse_x)
        lse_ref[0, rows, :] = lse_x


def _band_bucket(dilation, max_dist):
    i = jnp.arange(BAND)[:, None]
    jj = jnp.arange(2 * BAND)[None, :]
    delta = BAND + i - jj
    valid = (delta >= 0) & (delta <= max_dist)
    return jnp.where(valid, _t5_bucket(delta * dilation), -1).astype(I32)


def _band_attn(p, table, sinks, g):
    _, d, max_dist, has_sink = GROUPS[g]
    B, S, _ = p.shape
    L = S // d
    assert S % d == 0 and L % BAND == 0
    tq = min(ATTN_TILE, L)
    assert L % tq == 0
    nj = L // tq
    p3 = p.reshape(B, L, d * P_WIDTH)
    cb = P_WIDTH // GROUP_WIDTH
    q0 = g * 3
    sub = tq // BAND

    def cur(off):
        return pl.BlockSpec((1, tq, GROUP_WIDTH), lambda b, r, j: (b, j, r * cb + q0 + off))

    def prev(off):
        return pl.BlockSpec((1, BAND, GROUP_WIDTH),
                            lambda b, r, j: (b, jnp.maximum(j * sub - 1, 0), r * cb + q0 + off))

    smem = pl.BlockSpec(memory_space=pltpu.SMEM)
    o, lse = pl.pallas_call(
        functools.partial(_band_attn_kernel, tq=tq, has_sink=has_sink),
        grid=(B, d, nj),
        in_specs=[smem, smem,
                  pl.BlockSpec((BAND, 2 * BAND), lambda b, r, j: (0, 0)),
                  cur(0), cur(1), cur(2), prev(1), prev(2)],
        out_specs=[pl.BlockSpec((1, tq, GROUP_WIDTH), lambda b, r, j: (b, j, r)),
                   pl.BlockSpec((1, tq, GROUP_WIDTH), lambda b, r, j: (b, j, r))],
        out_shape=[jax.ShapeDtypeStruct((B, L, d * GROUP_WIDTH), BF16),
                   jax.ShapeDtypeStruct((B, L, d * GROUP_WIDTH), F32)],
        scratch_shapes=[pltpu.VMEM((GROUP_HEADS * BAND, 2 * BAND), F32),
                        pltpu.VMEM((GROUP_HEADS * BAND, 128), F32),
                        pltpu.VMEM((tq + BAND, GROUP_WIDTH), BF16),
                        pltpu.VMEM((tq + BAND, GROUP_WIDTH), BF16)],
        compiler_params=_cparams(3),
        name=f"band_attn_g{g}",
    )(table, sinks, _band_bucket(d, max_dist), p3, p3, p3, p3, p3)
    return o.reshape(B, S, GROUP_WIDTH), lse.reshape(B, S, GROUP_WIDTH)


def _merge_router_kernel(oa_ref, o1_ref, o2_ref, o3_ref, l1_ref, l2_ref, l3_ref, x_ref,
                         woa_ref, wod_ref, bo_ref, g_ref, be_ref, wr_ref, br_ref, base_ref,
                         x1_ref, meta_ref, cnt_ref, tri_scr, run_scr, *, tm, alpha):
    i = pl.program_id(0)

    @pl.when(i == 0)
    def _():
        rr = lax.broadcasted_iota(I32, (tm, tm), 0)
        cc = lax.broadcasted_iota(I32, (tm, tm), 1)
        tri_scr[...] = jnp.where(cc < rr, 1.0, 0.0).astype(BF16)
        run_scr[...] = base_ref[...]

    l1, l2, l3 = l1_ref[...], l2_ref[...], l3_ref[...]
    mx = jnp.maximum(jnp.maximum(l1, l2), l3)
    e1, e2, e3 = jnp.exp(l1 - mx), jnp.exp(l2 - mx), jnp.exp(l3 - mx)
    inv = 1.0 / (e1 + e2 + e3)
    od = (e1 * inv * o1_ref[...].astype(F32) + e2 * inv * o2_ref[...].astype(F32)
          + e3 * inv * o3_ref[...].astype(F32))
    mix = (jnp.dot(oa_ref[...], woa_ref[...], preferred_element_type=F32)
           + jnp.dot(od.astype(BF16), wod_ref[...], preferred_element_type=F32) + bo_ref[...])
    x1 = _layer_norm(alpha * x_ref[...] + mix, g_ref[...], be_ref[...])
    _lanes_to_rows(x1_ref, x1)

    logits = jnp.dot(x1, wr_ref[...], precision=lax.Precision.HIGHEST,
                     preferred_element_type=F32) + br_ref[...]
    lane = lax.broadcasted_iota(I32, (tm, N_EXPERTS), 1).astype(F32)
    sel = jnp.zeros((tm, N_EXPERTS), F32)
    vals, idxs = [], []
    l = logits
    for _ in range(TOP_K):
        v = jnp.max(l, axis=-1, keepdims=True)
        ik = jnp.min(jnp.where(l == v, lane, float(N_EXPERTS)), axis=-1, keepdims=True)
        hit = lane == ik
        l = jnp.where(hit, NEG_INF, l)
        sel = jnp.where(hit, 1.0, sel)
        vals.append(v)
        idxs.append(ik)
    evs = [jnp.exp(v - vals[0]) for v in vals]
    ginv = 1.0 / (evs[0] + evs[1] + evs[2] + evs[3])
    ahead = jnp.dot(tri_scr[...], sel.astype(BF16), preferred_element_type=F32) + run_scr[...]
    ranks = [jnp.sum(jnp.where(lane == ik, ahead, 0.0), axis=-1, keepdims=True) for ik in idxs]
    run_scr[...] = run_scr[...] + jnp.sum(sel, axis=0, keepdims=True)
    cnt_ref[...] = run_scr[...]

    mlane = lax.broadcasted_iota(I32, (tm, META_LANES), 1)
    meta = jnp.zeros((tm, META_LANES), F32)
    for k in range(TOP_K):
        meta = jnp.where(mlane == k, idxs[k], meta)
        meta = jnp.where(mlane == TOP_K + k, ranks[k], meta)
        meta = jnp.where(mlane == 2 * TOP_K + k, evs[k] * ginv, meta)
    meta_ref[...] = meta


def _merge_router(oa, o1, o2, o3, l1, l2, l3, x, woa, wod, bo, g, be, wr, br, base, alpha):
    T, D = x.shape
    tm = min(MERGE_TILE, T)
    assert T % tm == 0
    row = lambda w: pl.BlockSpec((tm, w), lambda i: (i, 0))
    full = lambda a: pl.BlockSpec(a.shape, lambda i: (0,) * a.ndim)
    return pl.pallas_call(
        functools.partial(_merge_router_kernel, tm=tm, alpha=alpha),
        grid=(T // tm,),
        in_specs=[row(GROUP_WIDTH)] * 7 + [row(D)] + [full(a) for a in (woa, wod, bo, g, be, wr, br, base)],
        out_specs=[pl.BlockSpec((tm, ROW_CHUNKS, 128), lambda i: (i, 0, 0)), row(META_LANES),
                   pl.BlockSpec((1, N_EXPERTS), lambda i: (0, 0))],
        out_shape=[jax.ShapeDtypeStruct((T, ROW_CHUNKS, 128), F32),
                   jax.ShapeDtypeStruct((T, META_LANES), F32),
                   jax.ShapeDtypeStruct((1, N_EXPERTS), F32)],
        scratch_shapes=[pltpu.VMEM((tm, tm), BF16), pltpu.VMEM((1, N_EXPERTS), F32)],
        compiler_params=_cparams(1),
        name="merge_router",
    )(oa, o1, o2, o3, l1, l2, l3, x, woa, wod, bo, g, be, wr, br, base)


def _row_copy(src_ref, s, dst_ref, t, sem):
    return pltpu.make_async_copy(src_ref.at[pl.ds(s, 1)], dst_ref.at[pl.ds(t, 1)], sem)


def _dispatch_kernel(zs_ref, pad_ref, used_ref, pos_ref, x_ref, *rest, td, fill_pad):
    if fill_pad:
        xg_ref, zero_scr, sem, zsem = rest
    else:
        _, xg_ref, zero_scr, sem, zsem = rest
    i = pl.program_id(0)

    if fill_pad:
        half = zero_scr.shape[0]

        def tail_copy(t):
            return pltpu.make_async_copy(zero_scr, xg_ref.at[pl.ds(t * half, half)], zsem)

        def tail_copies(fn):
            def body(t, c):
                fn(tail_copy(t))
                return c

            lax.fori_loop(used_ref[0] * (MOE_TILE // half), xg_ref.shape[0] // half, body, 0)

        bits = [1 << k for k in range(int(math.log2(MOE_TILE)) - 1, -1, -1)]

        def pad_copies(fn):
            for e in range(N_EXPERTS):
                pad = pad_ref[e]
                start = zs_ref[e]
                for bit in bits:
                    off = pad & ~(2 * bit - 1)

                    @pl.when((pad & bit) != 0)
                    def _(start=start, off=off, bit=bit):
                        fn(pltpu.make_async_copy(zero_scr.at[pl.ds(0, bit)],
                                                 xg_ref.at[pl.ds(start + off, bit)], zsem))

        @pl.when(i == 0)
        def _():
            zero_scr[...] = jnp.zeros(zero_scr.shape, zero_scr.dtype)
            pad_copies(lambda c: c.start())
            tail_copies(lambda c: c.start())
            pad_copies(lambda c: c.wait())
            tail_copies(lambda c: c.wait())

    def start_rows(r, c):
        for k in range(TOP_K):
            _row_copy(x_ref, r, xg_ref, pos_ref[r * TOP_K + k], sem).start()
        return c

    lax.fori_loop(0, td, start_rows, 0)

    def wait_rows(r, c):
        for k in range(TOP_K):
            _row_copy(x_ref, 0, xg_ref, 0, sem).wait()
        return c

    lax.fori_loop(0, td, wait_rows, 0)


def _dispatch(x1, pos_flat, zs, pad, used, n_rows, xg=None):
    T = x1.shape[0]
    td = min(DISPATCH_TILE, T)
    assert T % td == 0
    fill_pad = xg is None
    in_specs = [pl.BlockSpec((td * TOP_K,), lambda i, *_: (i,), memory_space=pltpu.SMEM),
                pl.BlockSpec((td, ROW_CHUNKS, 128), lambda i, *_: (i, 0, 0))]
    args = [pos_flat, x1]
    aliases = {}
    if not fill_pad:
        in_specs.append(pl.BlockSpec(memory_space=pl.ANY))
        args.append(xg)
        aliases = {5: 0}
    return pl.pallas_call(
        functools.partial(_dispatch_kernel, td=td, fill_pad=fill_pad),
        grid_spec=pltpu.PrefetchScalarGridSpec(
            num_scalar_prefetch=3,
            grid=(T // td,),
            in_specs=in_specs,
            out_specs=pl.BlockSpec(memory_space=pl.ANY),
            scratch_shapes=[pltpu.VMEM((MOE_TILE // 2, ROW_CHUNKS, 128), F32),
                            pltpu.SemaphoreType.DMA(()), pltpu.SemaphoreType.DMA(())]),
        out_shape=jax.ShapeDtypeStruct((n_rows, ROW_CHUNKS, 128), F32),
        input_output_aliases=aliases,
        compiler_params=_cparams(1),
        name="moe_dispatch",
    )(zs, pad, used, *args)


def _moe_kernel(te_ref, used_ref, xg_ref, wup_ref, bup_ref, wdn_ref, bdn_ref, y_ref,
                wup_bf, wdn_bf, acc_scr, *, ff, fc):
    i = pl.program_id(0)
    used = used_ref[0]
    e = te_ref[i]
    prev = te_ref[jnp.maximum(i - 1, 0)]
    d_model = wup_ref.shape[1]
    rc = 64

    @pl.when((i < used) & ((i == 0) | (e != prev)))
    def _():
        def cast_rows(r, c):
            rows = pl.ds(pl.multiple_of(r * rc, rc), rc)
            wup_bf[rows, :] = wup_ref[0, rows, :].astype(BF16)
            return c

        lax.fori_loop(0, d_model // rc, cast_rows, 0)

        def cast_rows_dn(r, c):
            rows = pl.ds(pl.multiple_of(r * rc, rc), rc)
            wdn_bf[rows, :] = wdn_ref[0, rows, :].astype(BF16)
            return c

        lax.fori_loop(0, ff // rc, cast_rows_dn, 0)

    @pl.when(i < used)
    def _():
        x = _rows_to_lanes(xg_ref).astype(BF16)
        for c in range(ff // fc):
            cg = slice(c * fc, (c + 1) * fc)
            cl = slice(ff + c * fc, ff + (c + 1) * fc)
            hg = jnp.dot(x, wup_bf[:, cg], preferred_element_type=F32) + bup_ref[0, :, cg]
            hl = jnp.dot(x, wup_bf[:, cl], preferred_element_type=F32) + bup_ref[0, :, cl]
            glu = jnp.minimum(hg, SWIGLU_LIMIT)
            lin = jnp.clip(hl, -SWIGLU_LIMIT, SWIGLU_LIMIT)
            a = glu * (1.0 / (1.0 + jnp.exp(-SWIGLU_ALPHA * glu))) * (lin + 1.0)
            yc = jnp.dot(a.astype(BF16), wdn_bf[cg, :], preferred_element_type=F32)
            if c == 0:
                acc_scr[...] = yc + bdn_ref[0]
            else:
                acc_scr[...] += yc
        _lanes_to_rows(y_ref, acc_scr[...])

    @pl.when(i >= used)
    def _():
        y_ref[...] = jnp.zeros(y_ref.shape, y_ref.dtype)


def _moe_experts(xg, tile_e, used, w_up, b_up, w_down, b_down, n_tiles):
    E, D, F2 = w_up.shape
    ff = F2 // 2
    tm = MOE_TILE
    fc = min(MOE_FF_CHUNK, ff)
    assert ff % fc == 0 and D % 64 == 0 and ff % 64 == 0
    return pl.pallas_call(
        functools.partial(_moe_kernel, ff=ff, fc=fc),
        grid_spec=pltpu.PrefetchScalarGridSpec(
            num_scalar_prefetch=2,
            grid=(n_tiles,),
            in_specs=[pl.BlockSpec((tm, ROW_CHUNKS, 128),
                                   lambda i, te, us: (jnp.minimum(i, us[0] - 1), 0, 0)),
                      pl.BlockSpec((1, D, F2), lambda i, te, us: (te[i], 0, 0)),
                      pl.BlockSpec((1, 1, F2), lambda i, te, us: (te[i], 0, 0)),
                      pl.BlockSpec((1, ff, D), lambda i, te, us: (te[i], 0, 0)),
                      pl.BlockSpec((1, 1, D), lambda i, te, us: (te[i], 0, 0))],
            out_specs=pl.BlockSpec((tm, ROW_CHUNKS, 128), lambda i, te, us: (i, 0, 0)),
            scratch_shapes=[pltpu.VMEM((D, F2), BF16), pltpu.VMEM((ff, D), BF16),
                            pltpu.VMEM((tm, D), F32)]),
        out_shape=jax.ShapeDtypeStruct((n_tiles * tm, ROW_CHUNKS, 128), F32),
        compiler_params=_cparams(1),
        name="moe_experts",
    )(tile_e, used, xg, w_up, b_up.reshape(E, 1, F2), w_down, b_down.reshape(E, 1, D))


def _combine_kernel(pos_ref, x1_ref, meta_ref, g_ref, b_ref, yg_ref, out_ref, buf, sem, *, tc, alpha):
    def start_rows(r, c):
        for k in range(TOP_K):
            pltpu.make_async_copy(yg_ref.at[pl.ds(pos_ref[r * TOP_K + k], 1)],
                                  buf.at[k, pl.ds(r, 1)], sem).start()
        return c

    lax.fori_loop(0, tc, start_rows, 0)

    def wait_rows(r, c):
        for k in range(TOP_K):
            pltpu.make_async_copy(yg_ref.at[pl.ds(0, 1)], buf.at[k, pl.ds(0, 1)], sem).wait()
        return c

    lax.fori_loop(0, tc, wait_rows, 0)
    meta = meta_ref[...]
    moe = jnp.zeros(out_ref.shape, F32)
    for k in range(TOP_K):
        moe = moe + meta[:, 2 * TOP_K + k: 2 * TOP_K + k + 1] * _rows_to_lanes(buf.at[k])
    out_ref[...] = _layer_norm(alpha * _rows_to_lanes(x1_ref) + moe, g_ref[...], b_ref[...])


def _combine(x1, meta, pos_flat, yg, g, b, alpha):
    T = x1.shape[0]
    D = ROW_CHUNKS * 128
    tc = min(COMBINE_TILE, T)
    assert T % tc == 0
    return pl.pallas_call(
        functools.partial(_combine_kernel, tc=tc, alpha=alpha),
        grid=(T // tc,),
        in_specs=[pl.BlockSpec((tc * TOP_K,), lambda i: (i,), memory_space=pltpu.SMEM),
                  pl.BlockSpec((tc, ROW_CHUNKS, 128), lambda i: (i, 0, 0)),
                  pl.BlockSpec((tc, META_LANES), lambda i: (i, 0)),
                  pl.BlockSpec((1, D), lambda i: (0, 0)),
                  pl.BlockSpec((1, D), lambda i: (0, 0)),
                  pl.BlockSpec(memory_space=pl.ANY)],
        out_specs=pl.BlockSpec((tc, D), lambda i: (i, 0)),
        out_shape=jax.ShapeDtypeStruct((T, D), F32),
        scratch_shapes=[pltpu.VMEM((TOP_K, tc, ROW_CHUNKS, 128), F32), pltpu.SemaphoreType.DMA(())],
        compiler_params=_cparams(1),
        name="moe_combine",
    )(pos_flat, x1, meta, g, b, yg)


def _qkv_sample_kernel(x_ref, wt_ref, bt_ref, pt_ref):
    xb = x_ref[...].astype(BF16)
    pt_ref[...] = lax.dot_general(wt_ref[...], xb, (((1,), (1,)), ((), ())),
                                  preferred_element_type=F32) + bt_ref[...]


def _qkv_sample(xs, wpt, bpt):
    n = xs.shape[0]
    return pl.pallas_call(
        _qkv_sample_kernel,
        out_shape=jax.ShapeDtypeStruct((P_WIDTH, n), F32),
        compiler_params=pltpu.CompilerParams(vmem_limit_bytes=VMEM_LIMIT),
        name="qkv_sample",
    )(xs, wpt, bpt)


def _sample_attn_kernel(table_ref, sink_ref, bucket_ref, pt_ref, cache_ref,
                        state_ref, ot_ref, lt_ref, bias_scr, *, bt, hk, win, has_sink):
    i = pl.program_id(0)
    H = GROUP_HEADS
    nb = pt_ref.shape[1]

    @pl.when(i == 0)
    def _():
        accs = _build_bias(bucket_ref[...], table_ref, H)
        for h in range(H):
            bias_scr[h:h + 1, :] = accs[h]
        ot_ref[...] = jnp.zeros(ot_ref.shape, F32)
        lt_ref[...] = jnp.zeros(lt_ref.shape, F32)

    lane_b = lax.broadcasted_iota(I32, (SLAB, nb), 1)
    out_lane = lax.broadcasted_iota(I32, (HEAD_DIM, nb), 1)
    t_lane = lax.broadcasted_iota(I32, (HEAD_DIM, win), 1)
    for bb in range(bt):
        b = i * bt + bb
        cols = jnp.sum(jnp.where(lane_b == b, pt_ref[...], 0.0), axis=1, keepdims=True)
        for j in range(H):
            kv = j % hk
            qc = cols[j * HEAD_DIM:(j + 1) * HEAD_DIM]
            kn = cols[GROUP_WIDTH + kv * HEAD_DIM: GROUP_WIDTH + (kv + 1) * HEAD_DIM]
            vn = cols[2 * GROUP_WIDTH + kv * HEAD_DIM: 2 * GROUP_WIDTH + (kv + 1) * HEAD_DIM]
            s = jnp.sum(cache_ref[bb, 0, kv] * qc, axis=0, keepdims=True) + bias_scr[j:j + 1, :]
            s_new = jnp.sum(qc * kn, axis=0, keepdims=True) + table_ref[j]
            m = jnp.maximum(jnp.max(s, axis=1, keepdims=True), s_new)
            if has_sink:
                m = jnp.maximum(m, sink_ref[j])
            e = jnp.exp(s - m)
            e_new = jnp.exp(s_new - m)
            den = jnp.sum(e, axis=1, keepdims=True) + e_new
            if has_sink:
                den = den + jnp.exp(sink_ref[j] - m)
            o = (jnp.sum(cache_ref[bb, 1, kv] * e, axis=1, keepdims=True) + vn * e_new) * (1.0 / den)
            lse = m + jnp.log(den)
            rows = slice(j * HEAD_DIM, (j + 1) * HEAD_DIM)
            ot_ref[rows, :] = jnp.where(out_lane == b, o, ot_ref[rows, :])
            lt_ref[rows, :] = jnp.where(out_lane == b, lse, lt_ref[rows, :])
        for part in range(2):
            for kv in range(hk):
                new = cols[(1 + part) * GROUP_WIDTH + kv * HEAD_DIM:
                           (1 + part) * GROUP_WIDTH + (kv + 1) * HEAD_DIM]
                rolled = pltpu.roll(cache_ref[bb, part, kv], win - 1, 1)
                state_ref[bb, part, kv] = jnp.where(t_lane == win - 1, new, rolled)


def _sample_bucket(win, d, n_off):
    t = jnp.arange(win)[None, :]
    dist = win - t
    valid = (dist % d == 0) & (dist // d <= n_off - 1)
    return jnp.where(valid, _t5_bucket(dist), -1).astype(I32)


def _sample_attn(cache_t, pt, table, sinks, g):
    win, d, _, has_sink = GROUPS[g]
    n, _, hk, _, w = cache_t.shape
    assert w == win, "cache must hold a full window"
    n_off = win // d + (0 if has_sink else 1)
    bt = max(1, min(n, (1 << 20) // (2 * hk * HEAD_DIM * win * 4)))
    while n % bt:
        bt -= 1
    smem = pl.BlockSpec(memory_space=pltpu.SMEM)
    blk = (bt, 2, hk, HEAD_DIM, win)
    return pl.pallas_call(
        functools.partial(_sample_attn_kernel, bt=bt, hk=hk, win=win, has_sink=has_sink),
        grid=(n // bt,),
        in_specs=[smem, smem,
                  pl.BlockSpec((1, win), lambda i: (0, 0)),
                  pl.BlockSpec((SLAB, n), lambda i: (g, 0)),
                  pl.BlockSpec(blk, lambda i: (i, 0, 0, 0, 0))],
        out_specs=[pl.BlockSpec(blk, lambda i: (i, 0, 0, 0, 0)),
                   pl.BlockSpec((GROUP_WIDTH, n), lambda i: (0, 0)),
                   pl.BlockSpec((GROUP_WIDTH, n), lambda i: (0, 0))],
        out_shape=[jax.ShapeDtypeStruct(cache_t.shape, F32),
                   jax.ShapeDtypeStruct((GROUP_WIDTH, n), F32),
                   jax.ShapeDtypeStruct((GROUP_WIDTH, n), F32)],
        scratch_shapes=[pltpu.VMEM((8, win), F32)],
        compiler_params=_cparams(1),
        name=f"sample_attn_g{g}",
    )(table, sinks, _sample_bucket(win, d, n_off), pt, cache_t)


def _projection_columns():
    swa_q = GROUP_WIDTH
    swa_kv = GROUP_WIDTH // 2
    dil_w = len(DIL_CONFIGS) * GROUP_WIDTH
    ka0, va0 = swa_q, swa_q + swa_kv
    qd0 = swa_q + 2 * swa_kv
    kd0, vd0 = qd0 + dil_w, qd0 + 2 * dil_w
    lane = np.arange(HEAD_DIM)
    cols = []
    cols += [h * HEAD_DIM + lane for h in SWA_HEAD_ORDER]
    cols += [ka0 + (j % 2) * HEAD_DIM + lane for j in range(GROUP_HEADS)]
    cols += [va0 + (j % 2) * HEAD_DIM + lane for j in range(GROUP_HEADS)]
    for g in range(len(DIL_CONFIGS)):
        for base in (qd0, kd0, vd0):
            cols.append(base + g * GROUP_WIDTH + np.arange(GROUP_WIDTH))
    cols = np.concatenate(cols)
    is_q = (np.arange(P_WIDTH) % SLAB) < GROUP_WIDTH
    return cols, is_q, (ka0, va0 + swa_kv)


def kernel(x_prompt, x_sample, cache_swa_kv, cache_dil1_kv, cache_dil2_kv, cache_dil3_kv,
           rel_bias_table, w_in, b_in, attn_sinks, w_o, b_o, ln1_g, ln1_b,
           w_router, b_router, w_up, b_up, w_down, b_down, ln2_g, ln2_b):
    depth = w_in.shape[0]
    assert depth == 1, "single-layer trunk"
    B, S, D = x_prompt.shape
    NS = x_sample.shape[0]
    assert x_sample.shape[1] == 1
    alpha = (2 * depth) ** 0.25

    cols, is_q, (kva0, kva1) = _projection_columns()
    scale = jnp.where(jnp.asarray(is_q), HEAD_DIM ** -0.5, 1.0).astype(F32)
    wp_f = w_in[0][:, cols] * scale
    bp_f = b_in[0][cols] * scale
    wp = wp_f.astype(BF16)
    bp = bp_f.reshape(1, P_WIDTH)
    wkva = w_in[0][:, kva0:kva1].astype(BF16)
    bkva = b_in[0][kva0:kva1].reshape(1, -1)
    swa_rows = np.concatenate([h * HEAD_DIM + np.arange(HEAD_DIM) for h in SWA_HEAD_ORDER])
    woa = w_o[0][swa_rows].astype(BF16)
    wod = w_o[0][GROUP_WIDTH:].astype(BF16)
    bo = b_o[0].reshape(1, D)
    tables = [rel_bias_table[:, np.asarray(SWA_HEAD_ORDER)].reshape(-1)]
    for g in range(len(DIL_CONFIGS)):
        tables.append(rel_bias_table[:, GROUP_HEADS * (1 + g): GROUP_HEADS * (2 + g)].reshape(-1))
    sinks = attn_sinks[0][np.asarray(SWA_HEAD_ORDER)]
    ln1 = (ln1_g[0].reshape(1, D), ln1_b[0].reshape(1, D))
    ln2 = (ln2_g[0].reshape(1, D), ln2_b[0].reshape(1, D))
    wr = w_router[0]
    br = b_router[0].reshape(1, N_EXPERTS)

    p, t_swa, t_d1, t_d2, t_d3 = _qkv_prompt(x_prompt, wp, bp, wkva, bkva)
    outs = [_band_attn(p, tables[g], sinks, g) for g in range(N_GROUPS)]
    T = B * S
    flat = lambda a: a.reshape(T, a.shape[-1])
    zero_cnt = jnp.zeros((1, N_EXPERTS), F32)
    x1_p, meta_p, cnt_p = _merge_router(
        flat(outs[0][0]), flat(outs[1][0]), flat(outs[2][0]), flat(outs[3][0]),
        flat(outs[1][1]), flat(outs[2][1]), flat(outs[3][1]), flat(x_prompt),
        woa, wod, bo, ln1[0], ln1[1], wr, br, zero_cnt, alpha)

    xs = x_sample.reshape(NS, D)
    pt = _qkv_sample(xs, wp.T, jnp.broadcast_to(bp_f[:, None], (P_WIDTH, NS)))
    caches = (cache_swa_kv, cache_dil1_kv, cache_dil2_kv, cache_dil3_kv)
    s_states, s_o, s_l = [], [], []
    for g, c in enumerate(caches):
        ct = jnp.transpose(c[0], (0, 2, 3, 4, 1))
        st, ot, lt = _sample_attn(ct, pt, tables[g], sinks, g)
        s_states.append(jnp.transpose(st, (0, 4, 1, 2, 3))[None])
        s_o.append(ot.T.astype(BF16))
        s_l.append(lt.T)
    x1_s, meta_s, cnt = _merge_router(
        s_o[0], s_o[1], s_o[2], s_o[3], s_l[1], s_l[2], s_l[3], xs,
        woa, wod, bo, ln1[0], ln1[1], wr, br, cnt_p, alpha)

    tm = MOE_TILE
    n_tiles = (T + NS) * TOP_K // tm + N_EXPERTS
    counts = cnt[0].astype(I32)
    padded = (counts + tm - 1) // tm * tm
    pend = jnp.cumsum(padded)
    pstart = pend - padded
    tile_e = jnp.minimum(jnp.searchsorted(pend, jnp.arange(n_tiles, dtype=I32) * tm, side="right"),
                         N_EXPERTS - 1).astype(I32)
    used = (pend[-1:] // tm).astype(I32)
    zs = (pstart + counts).astype(I32)
    pad = (padded - counts).astype(I32)

    def positions(meta):
        idx = meta[:, 0:TOP_K].astype(I32)
        rank = meta[:, TOP_K:2 * TOP_K].astype(I32)
        return (jnp.take(pstart, idx) + rank).reshape(-1).astype(I32)

    pos_p, pos_s = positions(meta_p), positions(meta_s)

    xg = _dispatch(x1_p, pos_p, zs, pad, used, n_tiles * tm)
    xg = _dispatch(x1_s, pos_s, zs, pad, used, n_tiles * tm, xg=xg)
    yg = _moe_experts(xg, tile_e, used, w_up[0], b_up[0], w_down[0], b_down[0], n_tiles)
    y_p = _combine(x1_p, meta_p, pos_p, yg, ln2[0], ln2[1], alpha).reshape(B, S, D)
    y_s = _combine(x1_s, meta_s, pos_s, yg, ln2[0], ln2[1], alpha).reshape(NS, 1, D)

    def prompt_state(t, heads):
        return t.reshape(1, B, t.shape[1], 2, heads, HEAD_DIM)

    return (y_p, y_s,
            prompt_state(t_swa, 2), prompt_state(t_d1, 4), prompt_state(t_d2, 4), prompt_state(t_d3, 4),
            s_states[0], s_states[1], s_states[2], s_states[3])
```
